```python
import jax, jax.numpy as jnp
from jax import lax
import numpy as np

D_MODEL = 1024
BATCH = 4
SEQ = 8192
DEPTH = 4

N_EVEN = (DEPTH + 1) // 2
N_ODD = DEPTH // 2
D_A = D_MODEL // 2
A_HEADS = 8
CONV_WIDTH_A = 3
POOL_WINDOWS = (2, 4, 8, 16)
D_B = D_MODEL // 2
B_GROUP = D_B // len(POOL_WINDOWS)
D_IN_AB = 3 * D_A + D_B
CONF_KERNEL = 31
D_FF = 256 * (-(-(8 * D_MODEL) // (3 * 256)))
N_MOD = 6
RMS_EPS = 1e-6
LN_EPS = 1e-5

kernel_name = "hybrid_conv_pool_conformer_encoder"


def rms_norm(x, g):
    xf = x.astype(jnp.float32)
    y = xf * lax.rsqrt(jnp.mean(xf * xf, axis=-1, keepdims=True) + RMS_EPS)
    return (y * g.astype(jnp.float32)).astype(x.dtype)


def layer_norm(x, g, b):
    xf = x.astype(jnp.float32)
    mu = jnp.mean(xf, axis=-1, keepdims=True)
    var = jnp.mean(jnp.square(xf - mu), axis=-1, keepdims=True)
    y = (xf - mu) * lax.rsqrt(var + LN_EPS)
    return (y * g.astype(jnp.float32) + b.astype(jnp.float32)).astype(x.dtype)


def modulate(h, shift, scale):
    return h * (1 + scale[:, None, :]) + shift[:, None, :]


def depthwise_conv(x, w):
    k = w.shape[0]
    left = (k - 1) // 2
    return lax.conv_general_dilated(
        x, w[:, None, :].astype(x.dtype), window_strides=(1,),
        padding=[(left, k - 1 - left)],
        dimension_numbers=('NWC', 'WIO', 'NWC'),
        feature_group_count=x.shape[-1])


def centred_window_mean(x, w):
    L = x.shape[1]
    left = w // 2
    right = w - 1 - left
    xp = jnp.pad(x.astype(jnp.float32), ((0, 0), (left + 1, right), (0, 0)))
    cs = jnp.cumsum(xp, axis=1)
    s = cs[:, w:w + L] - cs[:, :L]
    t = jnp.arange(L)
    cnt = (jnp.minimum(t + right, L - 1) - jnp.maximum(t - left, 0) + 1).astype(jnp.float32)
    return (s / cnt[None, :, None]).astype(x.dtype)


def conv_pool_mixer(h, w_in, conv_a, w_pool, pool_scale, w_out):
    u = jnp.einsum('bsd,de->bse', h, w_in)
    b_gate, c_gate, v, p = jnp.split(u, [D_A, 2 * D_A, 3 * D_A], axis=-1)
    y_a = b_gate * depthwise_conv(c_gate * v, conv_a)
    groups = jnp.split(p, len(POOL_WINDOWS), axis=-1)
    pooled = jnp.stack([centred_window_mean(g, w) - g for g, w in zip(groups, POOL_WINDOWS)],
                       axis=2)
    y_b = jnp.einsum('bsgc,gce->bsge', pooled, w_pool)
    y_b = y_b.reshape(h.shape[0], h.shape[1], D_B) * pool_scale
    return jnp.einsum('bse,ed->bsd', jnp.concatenate([y_a, y_b], axis=-1), w_out)


def conformer_conv(h, w_pw1, b_pw1, w_dw, b_dw, ln_g, ln_b, w_pw2, b_pw2):
    u = jnp.einsum('bsd,de->bse', h, w_pw1) + b_pw1
    a, g = jnp.split(u, 2, axis=-1)
    z = a * jax.nn.sigmoid(g)
    z = depthwise_conv(z, w_dw) + b_dw
    z = jax.nn.silu(layer_norm(z, ln_g, ln_b))
    return jnp.einsum('bsd,de->bse', z, w_pw2) + b_pw2


def swiglu(h, w_gate, w_up, w_down):
    a = jnp.einsum('bsd,df->bsf', h, w_gate)
    b = jnp.einsum('bsd,df->bsf', h, w_up)
    return jnp.einsum('bsf,fd->bsd', jax.nn.silu(a) * b, w_down)


def setup_inputs(seed: int = 0) -> dict:
    key = jax.random.key(seed)
    ks = iter(jax.random.split(key, 32))
    D = D_MODEL
    f32 = jnp.float32

    def nrm(shape, scale):
        return jax.random.normal(next(ks), shape, f32) * scale

    return {
        "x": nrm((BATCH, SEQ, D), 1.0),
        "c": nrm((BATCH, D), 1.0),
        "norm_mix_g": 1.0 + nrm((DEPTH, D), 0.05),
        "norm_ffn_g": 1.0 + nrm((DEPTH, D), 0.05),
        "w_mod": nrm((DEPTH, D, N_MOD * D), 0.5 * D ** -0.5),
        "b_mod": nrm((DEPTH, N_MOD * D), 0.02),
        "ab_w_in": nrm((N_EVEN, D, D_IN_AB), D ** -0.5),
        "ab_conv": nrm((N_EVEN, CONV_WIDTH_A, D_A), CONV_WIDTH_A ** -0.5),
        "ab_w_pool": nrm((N_EVEN, len(POOL_WINDOWS), B_GROUP, B_GROUP), B_GROUP ** -0.5),
        "ab_pool_scale": 1.0 + nrm((N_EVEN, D_B), 0.1),
        "ab_w_out": nrm((N_EVEN, D_A + D_B, D), (D_A + D_B) ** -0.5),
        "cf_w_pw1": nrm((N_ODD, D, 2 * D), D ** -0.5),
        "cf_b_pw1": nrm((N_ODD, 2 * D), 0.02),
        "cf_w_dw": nrm((N_ODD, CONF_KERNEL, D), CONF_KERNEL ** -0.5),
        "cf_b_dw": nrm((N_ODD, D), 0.02),
        "cf_ln_g": 1.0 + nrm((N_ODD, D), 0.05),
        "cf_ln_b": nrm((N_ODD, D), 0.02),
        "cf_w_pw2": nrm((N_ODD, D, D), D ** -0.5),
        "cf_b_pw2": nrm((N_ODD, D), 0.02),
        "ffn_w_gate": nrm((DEPTH, D, D_FF), D ** -0.5),
        "ffn_w_up": nrm((DEPTH, D, D_FF), D ** -0.5),
        "ffn_w_down": nrm((DEPTH, D_FF, D), D_FF ** -0.5),
        "final_norm_g": 1.0 + nrm((D,), 0.05),
    }


def reference(x, c, norm_mix_g, norm_ffn_g, w_mod, b_mod,
              ab_w_in, ab_conv, ab_w_pool, ab_pool_scale, ab_w_out,
              cf_w_pw1, cf_b_pw1, cf_w_dw, cf_b_dw, cf_ln_g, cf_ln_b, cf_w_pw2, cf_b_pw2,
              ffn_w_gate, ffn_w_up, ffn_w_down, final_norm_g):
    c_act = jax.nn.silu(c)
    for layer in range(DEPTH):
        mod = jnp.einsum('bd,de->be', c_act, w_mod[layer]) + b_mod[layer]
        sh1, sc1, g1, sh2, sc2, g2 = jnp.split(mod, N_MOD, axis=-1)
        h = modulate(rms_norm(x, norm_mix_g[layer]), sh1, sc1)
        i = layer // 2
        if layer % 2 == 0:
            y = conv_pool_mixer(h, ab_w_in[i], ab_conv[i], ab_w_pool[i],
                                ab_pool_scale[i], ab_w_out[i])
        else:
            y = conformer_conv(h, cf_w_pw1[i], cf_b_pw1[i], cf_w_dw[i], cf_b_dw[i],
                               cf_ln_g[i], cf_ln_b[i], cf_w_pw2[i], cf_b_pw2[i])
        x = x + g1[:, None, :] * y
        h = modulate(rms_norm(x, norm_ffn_g[layer]), sh2, sc2)
        x = x + g2[:, None, :] * swiglu(h, ffn_w_gate[layer], ffn_w_up[layer], ffn_w_down[layer])
    return rms_norm(x, final_norm_g)
```

```python
import functools

import jax
import jax.numpy as jnp
from jax import lax
from jax.experimental import pallas as pl
from jax.experimental.pallas import tpu as pltpu

RMS_EPS = 1e-6
LN_EPS = 1e-5
POOL_WINDOWS = (2, 4, 8, 16)
N_MOD = 6

VMEM_LIMIT_BYTES = 56 * 1024 * 1024
SUBLANES = 8
LANES = 128

ROW_TILE = 512
HALO_AB = 8
HALO_CF = 16
FF_CHUNK = 256


def _const_spec(shape):
    nd = len(shape)
    return pl.BlockSpec(shape, lambda b, i: (0,) * nd, pipeline_mode=pl.Buffered(1))


def _rms_mod(x, g, shift, scale):
    ms = jnp.mean(x * x, axis=-1, keepdims=True)
    y = x * lax.rsqrt(ms + RMS_EPS) * g
    return y * (1.0 + scale) + shift


def _sigmoid(x):
    return 1.0 / (1.0 + jnp.exp(-x))


def _halo_specs(tm, halo, seq, d):
    nb = tm // halo
    last = seq // halo - 1
    main = pl.BlockSpec((1, tm, d), lambda b, i: (b, i, 0))
    left = pl.BlockSpec((1, halo, d), lambda b, i: (b, jnp.maximum(i * nb - 1, 0), 0))
    right = pl.BlockSpec((1, halo, d), lambda b, i: (b, jnp.minimum((i + 1) * nb, last), 0))
    return main, left, right


def _mod_kernel(c_ref, w_ref, b_ref, o_ref):
    c = c_ref[...]
    c_act = c * _sigmoid(c)
    o_ref[0] = jnp.dot(c_act, w_ref[0], preferred_element_type=jnp.float32) + b_ref[0]


def _modulation(c, w_mod, b_mod):
    depth, d, n = w_mod.shape
    bsz = c.shape[0]
    tn = 1024
    return pl.pallas_call(
        _mod_kernel,
        grid=(depth, n // tn),
        in_specs=[
            pl.BlockSpec((bsz, d), lambda l, j: (0, 0)),
            pl.BlockSpec((1, d, tn), lambda l, j: (l, 0, j)),
            pl.BlockSpec((1, 1, tn), lambda l, j: (l, 0, j)),
        ],
        out_specs=pl.BlockSpec((1, bsz, tn), lambda l, j: (l, 0, j)),
        out_shape=jax.ShapeDtypeStruct((depth, bsz, n), jnp.float32),
        compiler_params=pltpu.CompilerParams(
            dimension_semantics=("arbitrary", "arbitrary"),
            vmem_limit_bytes=VMEM_LIMIT_BYTES),
        name="adaln_mod",
    )(c, w_mod, b_mod.reshape(depth, 1, n))


def _ab_kernel(xm_ref, xl_ref, xr_ref, mod_ref, g_ref, win_ref, conv_ref, wpool_ref,
               pscale_ref, wout_ref, o_ref, ycat_ref, *, tm, seq, d_a, b_group):
    i = pl.program_id(1)
    halo = HALO_AB
    rows = tm + 2 * halo
    xm = xm_ref[0]
    xe = jnp.concatenate([xl_ref[0], xm, xr_ref[0]], axis=0)
    m = mod_ref[0]
    h = _rms_mod(xe, g_ref[...], m[0:1], m[1:2]).astype(jnp.bfloat16)
    u = jnp.dot(h, win_ref[...], preferred_element_type=jnp.float32)

    row = lax.broadcasted_iota(jnp.int32, (rows, 1), 0) + (i * tm - halo)
    valid = (row >= 0) & (row < seq)

    cv = jnp.where(valid, u[:, d_a:2 * d_a] * u[:, 2 * d_a:3 * d_a], 0.0)
    cw = conv_ref[...]
    conv = (cw[0:1] * cv[halo - 1:halo - 1 + tm]
            + cw[1:2] * cv[halo:halo + tm]
            + cw[2:3] * cv[halo + 1:halo + 1 + tm])
    ycat_ref[:, 0:d_a] = (u[halo:halo + tm, 0:d_a] * conv).astype(jnp.bfloat16)

    t = lax.broadcasted_iota(jnp.int32, (tm, 1), 0) + i * tm
    for g, w in enumerate(POOL_WINDOWS):
        lo = 3 * d_a + g * b_group
        pg = jnp.where(valid, u[:, lo:lo + b_group], 0.0)
        left = w // 2
        right = w - 1 - left
        s = pg[halo - left:halo - left + tm]
        for j in range(-left + 1, right + 1):
            s = s + pg[halo + j:halo + j + tm]
        cnt = (jnp.minimum(t + right, seq - 1) - jnp.maximum(t - left, 0) + 1).astype(jnp.float32)
        pooled = s / cnt - pg[halo:halo + tm]
        yb = jnp.dot(pooled.astype(jnp.bfloat16), wpool_ref[g], preferred_element_type=jnp.float32)
        yb = yb * pscale_ref[:, g * b_group:(g + 1) * b_group]
        ycat_ref[:, d_a + g * b_group:d_a + (g + 1) * b_group] = yb.astype(jnp.bfloat16)

    y = jnp.dot(ycat_ref[...], wout_ref[...], preferred_element_type=jnp.float32)
    o_ref[0] = xm + m[2:3] * y


def _ab_mixer(x, modl, g, w_in, conv_a, w_pool, pool_scale, w_out, *, tm):
    bsz, seq, d = x.shape
    d_a = conv_a.shape[1]
    n_groups, b_group, _ = w_pool.shape
    d_in = w_in.shape[1]
    d_cat = w_out.shape[0]
    main, left, right = _halo_specs(tm, HALO_AB, seq, d)
    kern = functools.partial(_ab_kernel, tm=tm, seq=seq, d_a=d_a, b_group=b_group)
    return pl.pallas_call(
        kern,
        grid=(bsz, seq // tm),
        in_specs=[
            main, left, right,
            pl.BlockSpec((1, N_MOD, d), lambda b, i: (b, 0, 0)),
            _const_spec((1, d)),
            _const_spec((d, d_in)),
            _const_spec(conv_a.shape),
            _const_spec(w_pool.shape),
            _const_spec((1, n_groups * b_group)),
            _const_spec((d_cat, d)),
        ],
        out_specs=pl.BlockSpec((1, tm, d), lambda b, i: (b, i, 0)),
        out_shape=jax.ShapeDtypeStruct(x.shape, x.dtype),
        scratch_shapes=[pltpu.VMEM((tm, d_cat), jnp.bfloat16)],
        compiler_params=pltpu.CompilerParams(
            dimension_semantics=("parallel", "parallel"),
            vmem_limit_bytes=VMEM_LIMIT_BYTES),
        name="ab_mixer",
    )(x, x, x, modl, g.reshape(1, d), w_in, conv_a, w_pool, pool_scale.reshape(1, -1), w_out)


def _cf_kernel(xm_ref, xl_ref, xr_ref, mod_ref, g_ref, w1_ref, b1_ref, wdw_ref, bdw_ref,
               lng_ref, lnb_ref, w2_ref, b2_ref, o_ref, z_ref, c_ref, *, tm, seq, d, taps):
    i = pl.program_id(1)
    halo = HALO_CF
    rows = tm + 2 * halo
    reach = (taps - 1) // 2
    xm = xm_ref[0]
    xe = jnp.concatenate([xl_ref[0], xm, xr_ref[0]], axis=0)
    m = mod_ref[0]
    h = _rms_mod(xe, g_ref[...], m[0:1], m[1:2]).astype(jnp.bfloat16)
    u = jnp.dot(h, w1_ref[...], preferred_element_type=jnp.float32) + b1_ref[...]
    row = lax.broadcasted_iota(jnp.int32, (rows, 1), 0) + (i * tm - halo)
    valid = (row >= 0) & (row < seq)
    z_ref[...] = jnp.where(valid, u[:, :d] * _sigmoid(u[:, d:]), 0.0)

    rchunk = 4 * SUBLANES
    window = rchunk + 2 * halo
    first = halo - reach

    def conv_rows(r, carry):
        r0 = pl.multiple_of(r * rchunk, rchunk)
        for c0 in range(0, d, LANES):
            win = z_ref[pl.ds(r0, window), c0:c0 + LANES]
            acc = jnp.zeros((rchunk, LANES), jnp.float32)
            for s in range(SUBLANES):
                sh = win if s == 0 else pltpu.roll(win, window - s, axis=0)
                for q in range(window // SUBLANES):
                    k = SUBLANES * q + s - first
                    if 0 <= k < taps and SUBLANES * q + rchunk <= window:
                        acc = acc + sh[SUBLANES * q:SUBLANES * q + rchunk] * wdw_ref[k:k + 1, c0:c0 + LANES]
            c_ref[pl.ds(r0, rchunk), c0:c0 + LANES] = acc
        return carry

    lax.fori_loop(0, tm // rchunk, conv_rows, 0)

    cz = c_ref[...] + bdw_ref[...]
    mu = jnp.mean(cz, axis=-1, keepdims=True)
    dz = cz - mu
    var = jnp.mean(dz * dz, axis=-1, keepdims=True)
    ln = dz * lax.rsqrt(var + LN_EPS) * lng_ref[...] + lnb_ref[...]
    act = (ln * _sigmoid(ln)).astype(jnp.bfloat16)
    y = jnp.dot(act, w2_ref[...], preferred_element_type=jnp.float32) + b2_ref[...]
    o_ref[0] = xm + m[2:3] * y


def _cf_mixer(x, modl, g, w_pw1, b_pw1, w_dw, b_dw, ln_g, ln_b, w_pw2, b_pw2, *, tm):
    bsz, seq, d = x.shape
    taps = w_dw.shape[0]
    main, left, right = _halo_specs(tm, HALO_CF, seq, d)
    kern = functools.partial(_cf_kernel, tm=tm, seq=seq, d=d, taps=taps)
    rows = tm + 2 * HALO_CF
    return pl.pallas_call(
        kern,
        grid=(bsz, seq // tm),
        in_specs=[
            main, left, right,
            pl.BlockSpec((1, N_MOD, d), lambda b, i: (b, 0, 0)),
            _const_spec((1, d)),
            _const_spec((d, 2 * d)),
            _const_spec((1, 2 * d)),
            _const_spec((taps, d)),
            _const_spec((1, d)),
            _const_spec((1, d)),
            _const_spec((1, d)),
            _const_spec((d, d)),
            _const_spec((1, d)),
        ],
        out_specs=pl.BlockSpec((1, tm, d), lambda b, i: (b, i, 0)),
        out_shape=jax.ShapeDtypeStruct(x.shape, x.dtype),
        scratch_shapes=[pltpu.VMEM((rows, d), jnp.float32), pltpu.VMEM((tm, d), jnp.float32)],
        compiler_params=pltpu.CompilerParams(
            dimension_semantics=("parallel", "parallel"),
            vmem_limit_bytes=VMEM_LIMIT_BYTES),
        name="cf_mixer",
    )(x, x, x, modl, g.reshape(1, d), w_pw1, b_pw1.reshape(1, -1), w_dw, b_dw.reshape(1, d),
      ln_g.reshape(1, d), ln_b.reshape(1, d), w_pw2, b_pw2.reshape(1, d))


def _ffn_kernel(x_ref, mod_ref, g_ref, wg_ref, wu_ref, wd_ref, fg_ref, o_ref, h_ref, acc_ref,
                *, d_ff, final_norm):
    x = x_ref[0]
    m = mod_ref[0]
    h_ref[...] = _rms_mod(x, g_ref[...], m[3:4], m[4:5]).astype(jnp.bfloat16)
    acc_ref[...] = jnp.zeros_like(acc_ref)

    def chunk(j, carry):
        c0 = pl.multiple_of(j * FF_CHUNK, FF_CHUNK)
        h = h_ref[...]
        a = jnp.dot(h, wg_ref[:, pl.ds(c0, FF_CHUNK)], preferred_element_type=jnp.float32)
        b = jnp.dot(h, wu_ref[:, pl.ds(c0, FF_CHUNK)], preferred_element_type=jnp.float32)
        s = (a * _sigmoid(a) * b).astype(jnp.bfloat16)
        acc_ref[...] += jnp.dot(s, wd_ref[pl.ds(c0, FF_CHUNK), :], preferred_element_type=jnp.float32)
        return carry

    lax.fori_loop(0, d_ff // FF_CHUNK, chunk, 0)
    out = x + m[5:6] * acc_ref[...]
    if final_norm:
        ms = jnp.mean(out * out, axis=-1, keepdims=True)
        out = out * lax.rsqrt(ms + RMS_EPS) * fg_ref[...]
    o_ref[0] = out


def _ffn(x, modl, g, w_gate, w_up, w_down, final_g, *, tm, final_norm):
    bsz, seq, d = x.shape
    d_ff = w_gate.shape[1]
    kern = functools.partial(_ffn_kernel, d_ff=d_ff, final_norm=final_norm)
    return pl.pallas_call(
        kern,
        grid=(bsz, seq // tm),
        in_specs=[
            pl.BlockSpec((1, tm, d), lambda b, i: (b, i, 0)),
            pl.BlockSpec((1, N_MOD, d), lambda b, i: (b, 0, 0)),
            _const_spec((1, d)),
            _const_spec((d, d_ff)),
            _const_spec((d, d_ff)),
            _const_spec((d_ff, d)),
            _const_spec((1, d)),
        ],
        out_specs=pl.BlockSpec((1, tm, d), lambda b, i: (b, i, 0)),
        out_shape=jax.ShapeDtypeStruct(x.shape, x.dtype),
        scratch_shapes=[pltpu.VMEM((tm, d), jnp.bfloat16), pltpu.VMEM((tm, d), jnp.float32)],
        compiler_params=pltpu.CompilerParams(
            dimension_semantics=("parallel", "parallel"),
            vmem_limit_bytes=VMEM_LIMIT_BYTES),
        name="ffn",
    )(x, modl, g.reshape(1, d), w_gate, w_up, w_down, final_g.reshape(1, d))


def kernel(x, c, norm_mix_g, norm_ffn_g, w_mod, b_mod, ab_w_in, ab_conv, ab_w_pool, ab_pool_scale, ab_w_out, cf_w_pw1, cf_b_pw1, cf_w_dw, cf_b_dw, cf_ln_g, cf_ln_b, cf_w_pw2, cf_b_pw2, ffn_w_gate, ffn_w_up, ffn_w_down, final_norm_g):
    depth = w_mod.shape[0]
    bsz, seq, d = x.shape
    bf16 = jnp.bfloat16
    mod = _modulation(c, w_mod, b_mod).reshape(depth, bsz, N_MOD, d)
    for layer in range(depth):
        i = layer // 2
        if layer % 2 == 0:
            x = _ab_mixer(x, mod[layer], norm_mix_g[layer], ab_w_in[i].astype(bf16), ab_conv[i],
                          ab_w_pool[i].astype(bf16), ab_pool_scale[i], ab_w_out[i].astype(bf16),
                          tm=ROW_TILE)
        else:
            x = _cf_mixer(x, mod[layer], norm_mix_g[layer], cf_w_pw1[i].astype(bf16), cf_b_pw1[i],
                          cf_w_dw[i], cf_b_dw[i], cf_ln_g[i], cf_ln_b[i],
                          cf_w_pw2[i].astype(bf16), cf_b_pw2[i], tm=ROW_TILE)
        x = _ffn(x, mod[layer], norm_ffn_g[layer], ffn_w_gate[layer].astype(bf16),
                 ffn_w_up[layer].astype(bf16), ffn_w_down[layer].astype(bf16), final_norm_g,
                 tm=ROW_TILE, final_norm=(layer == depth - 1))
    return x
```

```python
import functools

import jax
import jax.numpy as jnp
from jax import lax
from jax.experimental import pallas as pl
from jax.experimental.pallas import tpu as pltpu

RMS_EPS = 1e-6
LN_EPS = 1e-5
POOL_WINDOWS = (2, 4, 8, 16)
N_MOD = 6

VMEM_LIMIT_BYTES = 56 * 1024 * 1024
SUBLANES = 8
LANES = 128

ROW_TILE = 512
FFN_ROW_TILE = 1024
HALO_AB = 8
HALO_CF = 16
CF_ROW_CHUNK = 128
FF_CHUNK = 256


def _const_spec(shape):
    nd = len(shape)
    return pl.BlockSpec(shape, lambda b, i: (0,) * nd, pipeline_mode=pl.Buffered(1))


def _rms_mod(x, g, shift, scale):
    ms = jnp.mean(x * x, axis=-1, keepdims=True)
    y = x * lax.rsqrt(ms + RMS_EPS) * g
    return y * (1.0 + scale) + shift


def _sigmoid(x):
    return 1.0 / (1.0 + jnp.exp(-x))


def _halo_specs(tm, halo, seq, d):
    nb = tm // halo
    last = seq // halo - 1
    main = pl.BlockSpec((1, tm, d), lambda b, i: (b, i, 0))
    left = pl.BlockSpec((1, halo, d), lambda b, i: (b, jnp.maximum(i * nb - 1, 0), 0))
    right = pl.BlockSpec((1, halo, d), lambda b, i: (b, jnp.minimum((i + 1) * nb, last), 0))
    return main, left, right


def _mod_kernel(c_ref, w_ref, b_ref, o_ref):
    c = c_ref[...]
    c_act = c * _sigmoid(c)
    o_ref[0] = jnp.dot(c_act, w_ref[0], preferred_element_type=jnp.float32) + b_ref[0]


def _modulation(c, w_mod, b_mod):
    depth, d, n = w_mod.shape
    bsz = c.shape[0]
    tn = 1024
    return pl.pallas_call(
        _mod_kernel,
        grid=(depth, n // tn),
        in_specs=[
            pl.BlockSpec((bsz, d), lambda l, j: (0, 0)),
            pl.BlockSpec((1, d, tn), lambda l, j: (l, 0, j)),
            pl.BlockSpec((1, 1, tn), lambda l, j: (l, 0, j)),
        ],
        out_specs=pl.BlockSpec((1, bsz, tn), lambda l, j: (l, 0, j)),
        out_shape=jax.ShapeDtypeStruct((depth, bsz, n), jnp.float32),
        compiler_params=pltpu.CompilerParams(
            dimension_semantics=("arbitrary", "arbitrary"),
            vmem_limit_bytes=VMEM_LIMIT_BYTES),
        name="adaln_mod",
    )(c, w_mod, b_mod.reshape(depth, 1, n))


def _ab_kernel(xm_ref, xl_ref, xr_ref, mod_ref, g_ref, win_ref, conv_ref, wpool_ref,
               pscale_ref, wout_ref, o_ref, ycat_ref, sh_ref, *, tm, seq, d_a, b_group):
    i = pl.program_id(1)
    halo = HALO_AB
    rows = tm + 2 * halo
    xm = xm_ref[0]
    xe = jnp.concatenate([xl_ref[0], xm, xr_ref[0]], axis=0)
    m = mod_ref[0]
    h = _rms_mod(xe, g_ref[...], m[0:1], m[1:2]).astype(jnp.bfloat16)
    u = jnp.dot(h, win_ref[...], preferred_element_type=jnp.float32)

    row = lax.broadcasted_iota(jnp.int32, (rows, 1), 0) + (i * tm - halo)
    valid = (row >= 0) & (row < seq)

    n_a = d_a // LANES
    for j in range(n_a):
        cvj = u[:, d_a + j * LANES:d_a + (j + 1) * LANES] * u[:, 2 * d_a + j * LANES:2 * d_a + (j + 1) * LANES]
        sh_ref[j] = jnp.where(valid, cvj, 0.0)
    for g in range(len(POOL_WINDOWS)):
        lo = 3 * d_a + g * b_group
        sh_ref[n_a + g] = jnp.where(valid, u[:, lo:lo + b_group], 0.0)

    for j in range(n_a):
        cols = slice(j * LANES, (j + 1) * LANES)
        conv = (conv_ref[0:1, cols] * sh_ref[j, halo - 1:halo - 1 + tm, :]
                + conv_ref[1:2, cols] * sh_ref[j, halo:halo + tm, :]
                + conv_ref[2:3, cols] * sh_ref[j, halo + 1:halo + 1 + tm, :])
        ycat_ref[:, cols] = (u[halo:halo + tm, cols] * conv).astype(jnp.bfloat16)

    t = lax.broadcasted_iota(jnp.int32, (tm, 1), 0) + i * tm
    for g, w in enumerate(POOL_WINDOWS):
        left = w // 2
        right = w - 1 - left
        s = sh_ref[n_a + g, halo - left:halo - left + tm, :]
        for j in range(-left + 1, right + 1):
            s = s + sh_ref[n_a + g, halo + j:halo + j + tm, :]
        cnt = (jnp.minimum(t + right, seq - 1) - jnp.maximum(t - left, 0) + 1).astype(jnp.float32)
        pooled = s / cnt - sh_ref[n_a + g, halo:halo + tm, :]
        yb = jnp.dot(pooled.astype(jnp.bfloat16), wpool_ref[g], preferred_element_type=jnp.float32)
        yb = yb * pscale_ref[:, g * b_group:(g + 1) * b_group]
        ycat_ref[:, d_a + g * b_group:d_a + (g + 1) * b_group] = yb.astype(jnp.bfloat16)

    y = jnp.dot(ycat_ref[...], wout_ref[...], preferred_element_type=jnp.float32)
    o_ref[0] = xm + m[2:3] * y


def _ab_mixer(x, modl, g, w_in, conv_a, w_pool, pool_scale, w_out, *, tm):
    bsz, seq, d = x.shape
    d_a = conv_a.shape[1]
    n_groups, b_group, _ = w_pool.shape
    d_in = w_in.shape[1]
    d_cat = w_out.shape[0]
    assert b_group == LANES and d_a % LANES == 0 and n_groups == len(POOL_WINDOWS)
    main, left, right = _halo_specs(tm, HALO_AB, seq, d)
    kern = functools.partial(_ab_kernel, tm=tm, seq=seq, d_a=d_a, b_group=b_group)
    return pl.pallas_call(
        kern,
        grid=(bsz, seq // tm),
        in_specs=[
            main, left, right,
            pl.BlockSpec((1, N_MOD, d), lambda b, i: (b, 0, 0)),
            _const_spec((1, d)),
            _const_spec((d, d_in)),
            _const_spec(conv_a.shape),
            _const_spec(w_pool.shape),
            _const_spec((1, n_groups * b_group)),
            _const_spec((d_cat, d)),
        ],
        out_specs=pl.BlockSpec((1, tm, d), lambda b, i: (b, i, 0)),
        out_shape=jax.ShapeDtypeStruct(x.shape, x.dtype),
        scratch_shapes=[pltpu.VMEM((tm, d_cat), jnp.bfloat16),
                        pltpu.VMEM((d_a // LANES + n_groups, tm + 2 * HALO_AB, LANES), jnp.float32)],
        compiler_params=pltpu.CompilerParams(
            dimension_semantics=("parallel", "parallel"),
            vmem_limit_bytes=VMEM_LIMIT_BYTES),
        name="ab_mixer",
    )(x, x, x, modl, g.reshape(1, d), w_in, conv_a, w_pool, pool_scale.reshape(1, -1), w_out)


def _cf_kernel(xm_ref, xl_ref, xr_ref, mod_ref, g_ref, w1_ref, b1_ref, wdw_ref, bdw_ref,
               lng_ref, lnb_ref, w2_ref, b2_ref, o_ref, z_ref, c_ref, *, tm, seq, d, taps):
    i = pl.program_id(1)
    halo = HALO_CF
    rows = tm + 2 * halo
    reach = (taps - 1) // 2
    first = halo - reach
    n_chunks = tm // CF_ROW_CHUNK
    n_cb = d // LANES
    m = mod_ref[0]

    def glu_rows(lo, hi):
        parts = []
        if lo < halo:
            parts.append(xl_ref[0, lo:halo, :])
        parts.append(xm_ref[0, max(lo - halo, 0):min(hi - halo, tm), :])
        if hi > halo + tm:
            parts.append(xr_ref[0, 0:hi - halo - tm, :])
        xe = parts[0] if len(parts) == 1 else jnp.concatenate(parts, axis=0)
        h = _rms_mod(xe, g_ref[...], m[0:1], m[1:2]).astype(jnp.bfloat16)
        u = jnp.dot(h, w1_ref[...], preferred_element_type=jnp.float32) + b1_ref[...]
        row = lax.broadcasted_iota(jnp.int32, (hi - lo, 1), 0) + (i * tm - halo + lo)
        valid = (row >= 0) & (row < seq)
        z = jnp.where(valid, u[:, :d] * _sigmoid(u[:, d:]), 0.0)
        for j in range(n_cb):
            z_ref[j, lo:hi, :] = z[:, j * LANES:(j + 1) * LANES]

    def conv_rows(lo):
        hi = lo + CF_ROW_CHUNK
        sub = 4 * SUBLANES
        for r0 in range(lo, hi, sub):
            for j in range(n_cb):
                acc = None
                for k in range(taps):
                    term = (z_ref[j, r0 + first + k:r0 + first + k + sub, :]
                            * wdw_ref[k:k + 1, j * LANES:(j + 1) * LANES])
                    acc = term if acc is None else acc + term
                c_ref[r0:r0 + sub, j * LANES:(j + 1) * LANES] = acc
        cz = c_ref[lo:hi, :] + bdw_ref[...]
        mu = jnp.mean(cz, axis=-1, keepdims=True)
        dz = cz - mu
        var = jnp.mean(dz * dz, axis=-1, keepdims=True)
        ln = dz * lax.rsqrt(var + LN_EPS) * lng_ref[...] + lnb_ref[...]
        act = (ln * _sigmoid(ln)).astype(jnp.bfloat16)
        y = jnp.dot(act, w2_ref[...], preferred_element_type=jnp.float32) + b2_ref[...]
        o_ref[0, lo:hi, :] = xm_ref[0, lo:hi, :] + m[2:3] * y

    bounds = [0] + [2 * halo + CF_ROW_CHUNK * (c + 1) for c in range(n_chunks - 1)] + [rows]
    glu_rows(bounds[0], bounds[1])
    for c in range(n_chunks):
        if c + 1 < n_chunks:
            glu_rows(bounds[c + 1], bounds[c + 2])
        conv_rows(c * CF_ROW_CHUNK)


def _cf_mixer(x, modl, g, w_pw1, b_pw1, w_dw, b_dw, ln_g, ln_b, w_pw2, b_pw2, *, tm):
    bsz, seq, d = x.shape
    taps = w_dw.shape[0]
    main, left, right = _halo_specs(tm, HALO_CF, seq, d)
    kern = functools.partial(_cf_kernel, tm=tm, seq=seq, d=d, taps=taps)
    rows = tm + 2 * HALO_CF
    return pl.pallas_call(
        kern,
        grid=(bsz, seq // tm),
        in_specs=[
            main, left, right,
            pl.BlockSpec((1, N_MOD, d), lambda b, i: (b, 0, 0)),
            _const_spec((1, d)),
            _const_spec((d, 2 * d)),
            _const_spec((1, 2 * d)),
            _const_spec((taps, d)),
            _const_spec((1, d)),
            _const_spec((1, d)),
            _const_spec((1, d)),
            _const_spec((d, d)),
            _const_spec((1, d)),
        ],
        out_specs=pl.BlockSpec((1, tm, d), lambda b, i: (b, i, 0)),
        out_shape=jax.ShapeDtypeStruct(x.shape, x.dtype),
        scratch_shapes=[pltpu.VMEM((d // LANES, rows, LANES), jnp.float32),
                        pltpu.VMEM((tm, d), jnp.float32)],
        compiler_params=pltpu.CompilerParams(
            dimension_semantics=("parallel", "parallel"),
            vmem_limit_bytes=VMEM_LIMIT_BYTES),
        name="cf_mixer",
    )(x, x, x, modl, g.reshape(1, d), w_pw1, b_pw1.reshape(1, -1), w_dw, b_dw.reshape(1, d),
      ln_g.reshape(1, d), ln_b.reshape(1, d), w_pw2, b_pw2.reshape(1, d))


def _ffn_kernel(x_ref, mod_ref, g_ref, wg_ref, wu_ref, wd_ref, fg_ref, o_ref, h_ref, s_ref,
                *, d_ff, final_norm):
    x = x_ref[0]
    m = mod_ref[0]
    h_ref[...] = _rms_mod(x, g_ref[...], m[3:4], m[4:5]).astype(jnp.bfloat16)
    for c0 in range(0, d_ff, FF_CHUNK):
        h = h_ref[...]
        a = jnp.dot(h, wg_ref[:, c0:c0 + FF_CHUNK], preferred_element_type=jnp.float32)
        b = jnp.dot(h, wu_ref[:, c0:c0 + FF_CHUNK], preferred_element_type=jnp.float32)
        s_ref[:, c0:c0 + FF_CHUNK] = (a * _sigmoid(a) * b).astype(jnp.bfloat16)
    y = jnp.dot(s_ref[...], wd_ref[...], preferred_element_type=jnp.float32)
    out = x + m[5:6] * y
    if final_norm:
        ms = jnp.mean(out * out, axis=-1, keepdims=True)
        out = out * lax.rsqrt(ms + RMS_EPS) * fg_ref[...]
    o_ref[0] = out


def _ffn(x, modl, g, w_gate, w_up, w_down, final_g, *, tm, final_norm):
    bsz, seq, d = x.shape
    d_ff = w_gate.shape[1]
    kern = functools.partial(_ffn_kernel, d_ff=d_ff, final_norm=final_norm)
    return pl.pallas_call(
        kern,
        grid=(bsz, seq // tm),
        in_specs=[
            pl.BlockSpec((1, tm, d), lambda b, i: (b, i, 0)),
            pl.BlockSpec((1, N_MOD, d), lambda b, i: (b, 0, 0)),
            _const_spec((1, d)),
            _const_spec((d, d_ff)),
            _const_spec((d, d_ff)),
            _const_spec((d_ff, d)),
            _const_spec((1, d)),
        ],
        out_specs=pl.BlockSpec((1, tm, d), lambda b, i: (b, i, 0)),
        out_shape=jax.ShapeDtypeStruct(x.shape, x.dtype),
        scratch_shapes=[pltpu.VMEM((tm, d), jnp.bfloat16), pltpu.VMEM((tm, d_ff), jnp.bfloat16)],
        compiler_params=pltpu.CompilerParams(
            dimension_semantics=("parallel", "parallel"),
            vmem_limit_bytes=VMEM_LIMIT_BYTES),
        name="ffn",
    )(x, modl, g.reshape(1, d), w_gate, w_up, w_down, final_g.reshape(1, d))


def kernel(x, c, norm_mix_g, norm_ffn_g, w_mod, b_mod, ab_w_in, ab_conv, ab_w_pool, ab_pool_scale, ab_w_out, cf_w_pw1, cf_b_pw1, cf_w_dw, cf_b_dw, cf_ln_g, cf_ln_b, cf_w_pw2, cf_b_pw2, ffn_w_gate, ffn_w_up, ffn_w_down, final_norm_g):
    depth = w_mod.shape[0]
    bsz, seq, d = x.shape
    bf16 = jnp.bfloat16
    mod = _modulation(c, w_mod, b_mod).reshape(depth, bsz, N_MOD, d)
    for layer in range(depth):
        i = layer // 2
        if layer % 2 == 0:
            x = _ab_mixer(x, mod[layer], norm_mix_g[layer], ab_w_in[i].astype(bf16), ab_conv[i],
                          ab_w_pool[i].astype(bf16), ab_pool_scale[i], ab_w_out[i].astype(bf16),
                          tm=ROW_TILE)
        else:
            x = _cf_mixer(x, mod[layer], norm_mix_g[layer], cf_w_pw1[i].astype(bf16), cf_b_pw1[i],
                          cf_w_dw[i], cf_b_dw[i], cf_ln_g[i], cf_ln_b[i],
                          cf_w_pw2[i].astype(bf16), cf_b_pw2[i], tm=ROW_TILE)
        x = _ffn(x, mod[layer], norm_ffn_g[layer], ffn_w_gate[layer].astype(bf16),
                 ffn_w_up[layer].astype(bf16), ffn_w_down[layer].astype(bf16), final_norm_g,
                 tm=FFN_ROW_TILE, final_norm=(layer == depth - 1))
    return x
```

```python
import functools

import jax
import jax.numpy as jnp
from jax import lax
from jax.experimental import pallas as pl
from jax.experimental.pallas import tpu as pltpu

RMS_EPS = 1e-6
LN_EPS = 1e-5
POOL_WINDOWS = (2, 4, 8, 16)
N_MOD = 6

VMEM_LIMIT_BYTES = 56 * 1024 * 1024
SUBLANES = 8
LANES = 128

AB_ROW_TILE = 1024
CF_ROW_TILE = 512
FFN_ROW_TILE = 1024
HALO_AB = 8
HALO_CF = 16
CF_ROW_CHUNK = 512
FF_CHUNK = 256


def _const_spec(shape):
    nd = len(shape)
    return pl.BlockSpec(shape, lambda b, i: (0,) * nd, pipeline_mode=pl.Buffered(1))


def _rms_mod(x, g, shift, scale):
    ms = jnp.mean(x * x, axis=-1, keepdims=True)
    y = x * lax.rsqrt(ms + RMS_EPS) * g
    return y * (1.0 + scale) + shift


def _sigmoid(x):
    return 1.0 / (1.0 + jnp.exp(-x))


def _halo_specs(tm, halo, seq, d):
    nb = tm // halo
    last = seq // halo - 1
    main = pl.BlockSpec((1, tm, d), lambda b, i: (b, i, 0))
    left = pl.BlockSpec((1, halo, d), lambda b, i: (b, jnp.maximum(i * nb - 1, 0), 0))
    right = pl.BlockSpec((1, halo, d), lambda b, i: (b, jnp.minimum((i + 1) * nb, last), 0))
    return main, left, right


def _mod_kernel(c_ref, w_ref, b_ref, o_ref):
    c = c_ref[...]
    c_act = c * _sigmoid(c)
    o_ref[0] = jnp.dot(c_act, w_ref[0], preferred_element_type=jnp.float32) + b_ref[0]


def _modulation(c, w_mod, b_mod):
    depth, d, n = w_mod.shape
    bsz = c.shape[0]
    tn = 1024
    return pl.pallas_call(
        _mod_kernel,
        grid=(depth, n // tn),
        in_specs=[
            pl.BlockSpec((bsz, d), lambda l, j: (0, 0)),
            pl.BlockSpec((1, d, tn), lambda l, j: (l, 0, j)),
            pl.BlockSpec((1, 1, tn), lambda l, j: (l, 0, j)),
        ],
        out_specs=pl.BlockSpec((1, bsz, tn), lambda l, j: (l, 0, j)),
        out_shape=jax.ShapeDtypeStruct((depth, bsz, n), jnp.float32),
        compiler_params=pltpu.CompilerParams(
            dimension_semantics=("arbitrary", "arbitrary"),
            vmem_limit_bytes=VMEM_LIMIT_BYTES),
        name="adaln_mod",
    )(c, w_mod, b_mod.reshape(depth, 1, n))


def _ab_kernel(xm_ref, xl_ref, xr_ref, mod_ref, g_ref, win_ref, conv_ref, wpool_ref,
               pscale_ref, wout_ref, o_ref, ycat_ref, sh_ref, *, tm, seq, d_a, b_group):
    i = pl.program_id(1)
    halo = HALO_AB
    rows = tm + 2 * halo
    xm = xm_ref[0]
    xe = jnp.concatenate([xl_ref[0], xm, xr_ref[0]], axis=0)
    m = mod_ref[0]
    h = _rms_mod(xe, g_ref[...], m[0:1], m[1:2]).astype(jnp.bfloat16)
    u = jnp.dot(h, win_ref[...], preferred_element_type=jnp.float32)

    row = lax.broadcasted_iota(jnp.int32, (rows, 1), 0) + (i * tm - halo)
    valid = (row >= 0) & (row < seq)

    n_a = d_a // LANES
    for j in range(n_a):
        cvj = u[:, d_a + j * LANES:d_a + (j + 1) * LANES] * u[:, 2 * d_a + j * LANES:2 * d_a + (j + 1) * LANES]
        sh_ref[j] = jnp.where(valid, cvj, 0.0)
    for g in range(len(POOL_WINDOWS)):
        lo = 3 * d_a + g * b_group
        sh_ref[n_a + g] = jnp.where(valid, u[:, lo:lo + b_group], 0.0)

    for j in range(n_a):
        cols = slice(j * LANES, (j + 1) * LANES)
        conv = (conv_ref[0:1, cols] * sh_ref[j, halo - 1:halo - 1 + tm, :]
                + conv_ref[1:2, cols] * sh_ref[j, halo:halo + tm, :]
                + conv_ref[2:3, cols] * sh_ref[j, halo + 1:halo + 1 + tm, :])
        ycat_ref[:, cols] = (u[halo:halo + tm, cols] * conv).astype(jnp.bfloat16)

    t = lax.broadcasted_iota(jnp.int32, (tm, 1), 0) + i * tm
    for g, w in enumerate(POOL_WINDOWS):
        left = w // 2
        right = w - 1 - left
        s = sh_ref[n_a + g, halo - left:halo - left + tm, :]
        for j in range(-left + 1, right + 1):
            s = s + sh_ref[n_a + g, halo + j:halo + j + tm, :]
        cnt = (jnp.minimum(t + right, seq - 1) - jnp.maximum(t - left, 0) + 1).astype(jnp.float32)
        pooled = s * (1.0 / cnt) - sh_ref[n_a + g, halo:halo + tm, :]
        yb = jnp.dot(pooled.astype(jnp.bfloat16), wpool_ref[g], preferred_element_type=jnp.float32)
        yb = yb * pscale_ref[:, g * b_group:(g + 1) * b_group]
        ycat_ref[:, d_a + g * b_group:d_a + (g + 1) * b_group] = yb.astype(jnp.bfloat16)

    y = jnp.dot(ycat_ref[...], wout_ref[...], preferred_element_type=jnp.float32)
    o_ref[0] = xm + m[2:3] * y


def _ab_mixer(x, modl, g, w_in, conv_a, w_pool, pool_scale, w_out, *, tm):
    bsz, seq, d = x.shape
    d_a = conv_a.shape[1]
    n_groups, b_group, _ = w_pool.shape
    d_in = w_in.shape[1]
    d_cat = w_out.shape[0]
    assert b_group == LANES and d_a % LANES == 0 and n_groups == len(POOL_WINDOWS)
    main, left, right = _halo_specs(tm, HALO_AB, seq, d)
    kern = functools.partial(_ab_kernel, tm=tm, seq=seq, d_a=d_a, b_group=b_group)
    return pl.pallas_call(
        kern,
        grid=(bsz, seq // tm),
        in_specs=[
            main, left, right,
            pl.BlockSpec((1, N_MOD, d), lambda b, i: (b, 0, 0)),
            _const_spec((1, d)),
            _const_spec((d, d_in)),
            _const_spec(conv_a.shape),
            _const_spec(w_pool.shape),
            _const_spec((1, n_groups * b_group)),
            _const_spec((d_cat, d)),
        ],
        out_specs=pl.BlockSpec((1, tm, d), lambda b, i: (b, i, 0)),
        out_shape=jax.ShapeDtypeStruct(x.shape, x.dtype),
        scratch_shapes=[pltpu.VMEM((tm, d_cat), jnp.bfloat16),
                        pltpu.VMEM((d_a // LANES + n_groups, tm + 2 * HALO_AB, LANES), jnp.float32)],
        compiler_params=pltpu.CompilerParams(
            dimension_semantics=("parallel", "parallel"),
            vmem_limit_bytes=VMEM_LIMIT_BYTES),
        name="ab_mixer",
    )(x, x, x, modl, g.reshape(1, d), w_in, conv_a, w_pool, pool_scale.reshape(1, -1), w_out)


def _cf_kernel(xm_ref, xl_ref, xr_ref, mod_ref, g_ref, w1_ref, b1_ref, wdw_ref, bdw_ref,
               lng_ref, lnb_ref, w2_ref, b2_ref, o_ref, z_ref, c_ref, *, tm, seq, d, taps):
    i = pl.program_id(1)
    halo = HALO_CF
    rows = tm + 2 * halo
    reach = (taps - 1) // 2
    first = halo - reach
    n_chunks = tm // CF_ROW_CHUNK
    n_cb = d // LANES
    m = mod_ref[0]

    def glu_rows(lo, hi):
        parts = []
        if lo < halo:
            parts.append(xl_ref[0, lo:halo, :])
        parts.append(xm_ref[0, max(lo - halo, 0):min(hi - halo, tm), :])
        if hi > halo + tm:
            parts.append(xr_ref[0, 0:hi - halo - tm, :])
        xe = parts[0] if len(parts) == 1 else jnp.concatenate(parts, axis=0)
        h = _rms_mod(xe, g_ref[...], m[0:1], m[1:2]).astype(jnp.bfloat16)
        u = jnp.dot(h, w1_ref[...], preferred_element_type=jnp.float32) + b1_ref[...]
        row = lax.broadcasted_iota(jnp.int32, (hi - lo, 1), 0) + (i * tm - halo + lo)
        valid = (row >= 0) & (row < seq)
        z = jnp.where(valid, u[:, :d] * _sigmoid(u[:, d:]), 0.0)
        for j in range(n_cb):
            z_ref[j, lo:hi, :] = z[:, j * LANES:(j + 1) * LANES]

    def conv_rows(lo):
        hi = lo + CF_ROW_CHUNK
        sub = 4 * SUBLANES
        for r0 in range(lo, hi, sub):
            for j in range(n_cb):
                acc = None
                for k in range(taps):
                    term = (z_ref[j, r0 + first + k:r0 + first + k + sub, :]
                            * wdw_ref[k:k + 1, j * LANES:(j + 1) * LANES])
                    acc = term if acc is None else acc + term
                c_ref[r0:r0 + sub, j * LANES:(j + 1) * LANES] = acc
        cz = c_ref[lo:hi, :] + bdw_ref[...]
        mu = jnp.mean(cz, axis=-1, keepdims=True)
        dz = cz - mu
        var = jnp.mean(dz * dz, axis=-1, keepdims=True)
        ln = dz * lax.rsqrt(var + LN_EPS) * lng_ref[...] + lnb_ref[...]
        act = (ln * _sigmoid(ln)).astype(jnp.bfloat16)
        y = jnp.dot(act, w2_ref[...], preferred_element_type=jnp.float32) + b2_ref[...]
        o_ref[0, lo:hi, :] = xm_ref[0, lo:hi, :] + m[2:3] * y

    bounds = [0] + [2 * halo + CF_ROW_CHUNK * (c + 1) for c in range(n_chunks - 1)] + [rows]
    glu_rows(bounds[0], bounds[1])
    for c in range(n_chunks):
        if c + 1 < n_chunks:
            glu_rows(bounds[c + 1], bounds[c + 2])
        conv_rows(c * CF_ROW_CHUNK)


def _cf_mixer(x, modl, g, w_pw1, b_pw1, w_dw, b_dw, ln_g, ln_b, w_pw2, b_pw2, *, tm):
    bsz, seq, d = x.shape
    taps = w_dw.shape[0]
    main, left, right = _halo_specs(tm, HALO_CF, seq, d)
    kern = functools.partial(_cf_kernel, tm=tm, seq=seq, d=d, taps=taps)
    rows = tm + 2 * HALO_CF
    return pl.pallas_call(
        kern,
        grid=(bsz, seq // tm),
        in_specs=[
            main, left, right,
            pl.BlockSpec((1, N_MOD, d), lambda b, i: (b, 0, 0)),
            _const_spec((1, d)),
            _const_spec((d, 2 * d)),
            _const_spec((1, 2 * d)),
            _const_spec((taps, d)),
            _const_spec((1, d)),
            _const_spec((1, d)),
            _const_spec((1, d)),
            _const_spec((d, d)),
            _const_spec((1, d)),
        ],
        out_specs=pl.BlockSpec((1, tm, d), lambda b, i: (b, i, 0)),
        out_shape=jax.ShapeDtypeStruct(x.shape, x.dtype),
        scratch_shapes=[pltpu.VMEM((d // LANES, rows, LANES), jnp.float32),
                        pltpu.VMEM((tm, d), jnp.float32)],
        compiler_params=pltpu.CompilerParams(
            dimension_semantics=("parallel", "parallel"),
            vmem_limit_bytes=VMEM_LIMIT_BYTES),
        name="cf_mixer",
    )(x, x, x, modl, g.reshape(1, d), w_pw1, b_pw1.reshape(1, -1), w_dw, b_dw.reshape(1, d),
      ln_g.reshape(1, d), ln_b.reshape(1, d), w_pw2, b_pw2.reshape(1, d))


def _ffn_kernel(x_ref, mod_ref, g_ref, wg_ref, wu_ref, wd_ref, fg_ref, o_ref, h_ref, s_ref,
                *, d_ff, final_norm):
    x = x_ref[0]
    m = mod_ref[0]
    h_ref[...] = _rms_mod(x, g_ref[...], m[3:4], m[4:5]).astype(jnp.bfloat16)
    for c0 in range(0, d_ff, FF_CHUNK):
        h = h_ref[...]
        a = jnp.dot(h, wg_ref[:, c0:c0 + FF_CHUNK], preferred_element_type=jnp.float32)
        b = jnp.dot(h, wu_ref[:, c0:c0 + FF_CHUNK], preferred_element_type=jnp.float32)
        s_ref[:, c0:c0 + FF_CHUNK] = (a * _sigmoid(a) * b).astype(jnp.bfloat16)
    y = jnp.dot(s_ref[...], wd_ref[...], preferred_element_type=jnp.float32)
    out = x + m[5:6] * y
    if final_norm:
        ms = jnp.mean(out * out, axis=-1, keepdims=True)
        out = out * lax.rsqrt(ms + RMS_EPS) * fg_ref[...]
    o_ref[0] = out


def _ffn(x, modl, g, w_gate, w_up, w_down, final_g, *, tm, final_norm):
    bsz, seq, d = x.shape
    d_ff = w_gate.shape[1]
    kern = functools.partial(_ffn_kernel, d_ff=d_ff, final_norm=final_norm)
    return pl.pallas_call(
        kern,
        grid=(bsz, seq // tm),
        in_specs=[
            pl.BlockSpec((1, tm, d), lambda b, i: (b, i, 0)),
            pl.BlockSpec((1, N_MOD, d), lambda b, i: (b, 0, 0)),
            _const_spec((1, d)),
            _const_spec((d, d_ff)),
            _const_spec((d, d_ff)),
            _const_spec((d_ff, d)),
            _const_spec((1, d)),
        ],
        out_specs=pl.BlockSpec((1, tm, d), lambda b, i: (b, i, 0)),
        out_shape=jax.ShapeDtypeStruct(x.shape, x.dtype),
        scratch_shapes=[pltpu.VMEM((tm, d), jnp.bfloat16), pltpu.VMEM((tm, d_ff), jnp.bfloat16)],
        compiler_params=pltpu.CompilerParams(
            dimension_semantics=("parallel", "parallel"),
            vmem_limit_bytes=VMEM_LIMIT_BYTES),
        name="ffn",
    )(x, modl, g.reshape(1, d), w_gate, w_up, w_down, final_g.reshape(1, d))


def kernel(x, c, norm_mix_g, norm_ffn_g, w_mod, b_mod, ab_w_in, ab_conv, ab_w_pool, ab_pool_scale, ab_w_out, cf_w_pw1, cf_b_pw1, cf_w_dw, cf_b_dw, cf_ln_g, cf_ln_b, cf_w_pw2, cf_b_pw2, ffn_w_gate, ffn_w_up, ffn_w_down, final_norm_g):
    depth = w_mod.shape[0]
    bsz, seq, d = x.shape
    bf16 = jnp.bfloat16
    mod = _modulation(c, w_mod, b_mod).reshape(depth, bsz, N_MOD, d)
    for layer in range(depth):
        i = layer // 2
        if layer % 2 == 0:
            x = _ab_mixer(x, mod[layer], norm_mix_g[layer], ab_w_in[i].astype(bf16), ab_conv[i],
                          ab_w_pool[i].astype(bf16), ab_pool_scale[i], ab_w_out[i].astype(bf16),
                          tm=AB_ROW_TILE)
        else:
            x = _cf_mixer(x, mod[layer], norm_mix_g[layer], cf_w_pw1[i].astype(bf16), cf_b_pw1[i],
                          cf_w_dw[i], cf_b_dw[i], cf_ln_g[i], cf_ln_b[i],
                          cf_w_pw2[i].astype(bf16), cf_b_pw2[i], tm=CF_ROW_TILE)
        x = _ffn(x, mod[layer], norm_ffn_g[layer], ffn_w_gate[layer].astype(bf16),
                 ffn_w_up[layer].astype(bf16), ffn_w_down[layer].astype(bf16), final_norm_g,
                 tm=FFN_ROW_TILE, final_norm=(layer == depth - 1))
    return x
```

```python
import functools

import jax
import jax.numpy as jnp
from jax import lax
from jax.experimental import pallas as pl
from jax.experimental.pallas import tpu as pltpu

RMS_EPS = 1e-6
LN_EPS = 1e-5
POOL_WINDOWS = (2, 4, 8, 16)
N_MOD = 6

VMEM_LIMIT_BYTES = 56 * 1024 * 1024
SUBLANES = 8
LANES = 128

AB_ROW_TILE = 1024
CF_ROW_TILE = 512
FFN_ROW_TILE = 1024
HALO_AB = 8
HALO_CF = 16
CF_ROW_CHUNK = 512
FF_CHUNK = 256


def _const_spec(shape):
    nd = len(shape)
    return pl.BlockSpec(shape, lambda b, i: (0,) * nd, pipeline_mode=pl.Buffered(1))


def _layer_spec(stacked, index):
    tail = stacked.shape[1:]
    return pl.BlockSpec((None,) + tail, lambda b, i: (index,) + (0,) * len(tail),
                        pipeline_mode=pl.Buffered(1))


def _rms_mod(x, g, shift, scale):
    ms = jnp.mean(x * x, axis=-1, keepdims=True)
    return x * lax.rsqrt(ms + RMS_EPS) * (g * (1.0 + scale)) + shift


NEG_LOG2_E = -1.4426950408889634


def _sigmoid(x):
    return 1.0 / (1.0 + jnp.exp2(x * NEG_LOG2_E))


def _halo_specs(tm, halo, seq, d):
    nb = tm // halo
    last = seq // halo - 1
    main = pl.BlockSpec((1, tm, d), lambda b, i: (b, i, 0))
    left = pl.BlockSpec((1, halo, d), lambda b, i: (b, jnp.maximum(i * nb - 1, 0), 0))
    right = pl.BlockSpec((1, halo, d), lambda b, i: (b, jnp.minimum((i + 1) * nb, last), 0))
    return main, left, right


def _mod_kernel(c_ref, w_ref, b_ref, o_ref):
    c = c_ref[...]
    c_act = c * _sigmoid(c)
    o_ref[0] = jnp.dot(c_act, w_ref[0], preferred_element_type=jnp.float32) + b_ref[0]


def _modulation(c, w_mod, b_mod):
    depth, d, n = w_mod.shape
    bsz = c.shape[0]
    tn = 1024
    return pl.pallas_call(
        _mod_kernel,
        grid=(depth, n // tn),
        in_specs=[
            pl.BlockSpec((bsz, d), lambda l, j: (0, 0)),
            pl.BlockSpec((1, d, tn), lambda l, j: (l, 0, j)),
            pl.BlockSpec((1, 1, tn), lambda l, j: (l, 0, j)),
        ],
        out_specs=pl.BlockSpec((1, bsz, tn), lambda l, j: (l, 0, j)),
        out_shape=jax.ShapeDtypeStruct((depth, bsz, n), jnp.float32),
        compiler_params=pltpu.CompilerParams(
            dimension_semantics=("arbitrary", "arbitrary"),
            vmem_limit_bytes=VMEM_LIMIT_BYTES),
        name="adaln_mod",
    )(c, w_mod, b_mod.reshape(depth, 1, n))


def _ab_kernel(xm_ref, xl_ref, xr_ref, mod_ref, g_ref, win_ref, conv_ref, wpool_ref,
               pscale_ref, wout_ref, o_ref, ycat_ref, sh_ref, *, tm, seq, d_a, b_group):
    i = pl.program_id(1)
    halo = HALO_AB
    rows = tm + 2 * halo
    xm = xm_ref[0]
    xe = jnp.concatenate([xl_ref[0], xm, xr_ref[0]], axis=0)
    m = mod_ref[0]
    h = _rms_mod(xe, g_ref[...], m[0:1], m[1:2]).astype(jnp.bfloat16)
    u = jnp.dot(h, win_ref[...], preferred_element_type=jnp.float32)

    row = lax.broadcasted_iota(jnp.int32, (rows, 1), 0) + (i * tm - halo)
    valid = (row >= 0) & (row < seq)

    n_a = d_a // LANES
    for j in range(n_a):
        cvj = u[:, d_a + j * LANES:d_a + (j + 1) * LANES] * u[:, 2 * d_a + j * LANES:2 * d_a + (j + 1) * LANES]
        sh_ref[j] = jnp.where(valid, cvj, 0.0)
    for g in range(len(POOL_WINDOWS)):
        lo = 3 * d_a + g * b_group
        sh_ref[n_a + g] = jnp.where(valid, u[:, lo:lo + b_group], 0.0)

    for j in range(n_a):
        cols = slice(j * LANES, (j + 1) * LANES)
        conv = (conv_ref[0:1, cols] * sh_ref[j, halo - 1:halo - 1 + tm, :]
                + conv_ref[1:2, cols] * sh_ref[j, halo:halo + tm, :]
                + conv_ref[2:3, cols] * sh_ref[j, halo + 1:halo + 1 + tm, :])
        ycat_ref[:, cols] = (u[halo:halo + tm, cols] * conv).astype(jnp.bfloat16)

    t = lax.broadcasted_iota(jnp.int32, (tm, 1), 0) + i * tm
    for g, w in enumerate(POOL_WINDOWS):
        left = w // 2
        right = w - 1 - left
        s = sh_ref[n_a + g, halo - left:halo - left + tm, :]
        for j in range(-left + 1, right + 1):
            s = s + sh_ref[n_a + g, halo + j:halo + j + tm, :]
        cnt = (jnp.minimum(t + right, seq - 1) - jnp.maximum(t - left, 0) + 1).astype(jnp.float32)
        pooled = s * (1.0 / cnt) - sh_ref[n_a + g, halo:halo + tm, :]
        yb = jnp.dot(pooled.astype(jnp.bfloat16), wpool_ref[g], preferred_element_type=jnp.float32)
        yb = yb * pscale_ref[:, g * b_group:(g + 1) * b_group]
        ycat_ref[:, d_a + g * b_group:d_a + (g + 1) * b_group] = yb.astype(jnp.bfloat16)

    y = jnp.dot(ycat_ref[...], wout_ref[...], preferred_element_type=jnp.float32)
    o_ref[0] = xm + m[2:3] * y


def _ab_mixer(x, modl, g, w_in, conv_a, w_pool, pool_scale, w_out, index, *, tm):
    bsz, seq, d = x.shape
    d_a = conv_a.shape[1]
    _, n_groups, b_group, _ = w_pool.shape
    d_cat = w_out.shape[1]
    assert b_group == LANES and d_a % LANES == 0 and n_groups == len(POOL_WINDOWS)
    main, left, right = _halo_specs(tm, HALO_AB, seq, d)
    kern = functools.partial(_ab_kernel, tm=tm, seq=seq, d_a=d_a, b_group=b_group)
    return pl.pallas_call(
        kern,
        grid=(bsz, seq // tm),
        in_specs=[
            main, left, right,
            pl.BlockSpec((1, N_MOD, d), lambda b, i: (b, 0, 0)),
            _const_spec((1, d)),
            _layer_spec(w_in, index),
            _const_spec(conv_a.shape),
            _layer_spec(w_pool, index),
            _const_spec((1, n_groups * b_group)),
            _layer_spec(w_out, index),
        ],
        out_specs=pl.BlockSpec((1, tm, d), lambda b, i: (b, i, 0)),
        out_shape=jax.ShapeDtypeStruct(x.shape, x.dtype),
        scratch_shapes=[pltpu.VMEM((tm, d_cat), jnp.bfloat16),
                        pltpu.VMEM((d_a // LANES + n_groups, tm + 2 * HALO_AB, LANES), jnp.float32)],
        compiler_params=pltpu.CompilerParams(
            dimension_semantics=("parallel", "parallel"),
            vmem_limit_bytes=VMEM_LIMIT_BYTES),
        name="ab_mixer",
    )(x, x, x, modl, g.reshape(1, d), w_in, conv_a, w_pool, pool_scale.reshape(1, -1), w_out)


def _cf_kernel(xm_ref, xl_ref, xr_ref, mod_ref, g_ref, w1_ref, b1_ref, wdw_ref, bdw_ref,
               lng_ref, lnb_ref, w2_ref, b2_ref, o_ref, z_ref, c_ref, *, tm, seq, d, taps):
    i = pl.program_id(1)
    halo = HALO_CF
    rows = tm + 2 * halo
    reach = (taps - 1) // 2
    first = halo - reach
    n_chunks = tm // CF_ROW_CHUNK
    n_cb = d // LANES
    m = mod_ref[0]

    def glu_rows(lo, hi):
        parts = []
        if lo < halo:
            parts.append(xl_ref[0, lo:halo, :])
        parts.append(xm_ref[0, max(lo - halo, 0):min(hi - halo, tm), :])
        if hi > halo + tm:
            parts.append(xr_ref[0, 0:hi - halo - tm, :])
        xe = parts[0] if len(parts) == 1 else jnp.concatenate(parts, axis=0)
        h = _rms_mod(xe, g_ref[...], m[0:1], m[1:2]).astype(jnp.bfloat16)
        u = jnp.dot(h, w1_ref[...], preferred_element_type=jnp.float32) + b1_ref[...]
        row = lax.broadcasted_iota(jnp.int32, (hi - lo, 1), 0) + (i * tm - halo + lo)
        valid = (row >= 0) & (row < seq)
        z = jnp.where(valid, u[:, :d] * _sigmoid(u[:, d:]), 0.0)
        for j in range(n_cb):
            z_ref[j, lo:hi, :] = z[:, j * LANES:(j + 1) * LANES]

    def conv_rows(lo):
        hi = lo + CF_ROW_CHUNK
        sub = 4 * SUBLANES
        for r0 in range(lo, hi, sub):
            for j in range(n_cb):
                acc = None
                for k in range(taps):
                    term = (z_ref[j, r0 + first + k:r0 + first + k + sub, :]
                            * wdw_ref[k:k + 1, j * LANES:(j + 1) * LANES])
                    acc = term if acc is None else acc + term
                c_ref[r0:r0 + sub, j * LANES:(j + 1) * LANES] = acc
        cz = c_ref[lo:hi, :] + bdw_ref[...]
        mu = jnp.mean(cz, axis=-1, keepdims=True)
        dz = cz - mu
        var = jnp.mean(dz * dz, axis=-1, keepdims=True)
        ln = dz * lax.rsqrt(var + LN_EPS) * lng_ref[...] + lnb_ref[...]
        act = (ln * _sigmoid(ln)).astype(jnp.bfloat16)
        y = jnp.dot(act, w2_ref[...], preferred_element_type=jnp.float32) + b2_ref[...]
        o_ref[0, lo:hi, :] = xm_ref[0, lo:hi, :] + m[2:3] * y

    bounds = [0] + [2 * halo + CF_ROW_CHUNK * (c + 1) for c in range(n_chunks - 1)] + [rows]
    glu_rows(bounds[0], bounds[1])
    for c in range(n_chunks):
        if c + 1 < n_chunks:
            glu_rows(bounds[c + 1], bounds[c + 2])
        conv_rows(c * CF_ROW_CHUNK)


def _cf_mixer(x, modl, g, w_pw1, b_pw1, w_dw, b_dw, ln_g, ln_b, w_pw2, b_pw2, index, *, tm):
    bsz, seq, d = x.shape
    taps = w_dw.shape[0]
    main, left, right = _halo_specs(tm, HALO_CF, seq, d)
    kern = functools.partial(_cf_kernel, tm=tm, seq=seq, d=d, taps=taps)
    rows = tm + 2 * HALO_CF
    return pl.pallas_call(
        kern,
        grid=(bsz, seq // tm),
        in_specs=[
            main, left, right,
            pl.BlockSpec((1, N_MOD, d), lambda b, i: (b, 0, 0)),
            _const_spec((1, d)),
            _layer_spec(w_pw1, index),
            _const_spec((1, 2 * d)),
            _const_spec((taps, d)),
            _const_spec((1, d)),
            _const_spec((1, d)),
            _const_spec((1, d)),
            _layer_spec(w_pw2, index),
            _const_spec((1, d)),
        ],
        out_specs=pl.BlockSpec((1, tm, d), lambda b, i: (b, i, 0)),
        out_shape=jax.ShapeDtypeStruct(x.shape, x.dtype),
        scratch_shapes=[pltpu.VMEM((d // LANES, rows, LANES), jnp.float32),
                        pltpu.VMEM((tm, d), jnp.float32)],
        compiler_params=pltpu.CompilerParams(
            dimension_semantics=("parallel", "parallel"),
            vmem_limit_bytes=VMEM_LIMIT_BYTES),
        name="cf_mixer",
    )(x, x, x, modl, g.reshape(1, d), w_pw1, b_pw1.reshape(1, -1), w_dw, b_dw.reshape(1, d),
      ln_g.reshape(1, d), ln_b.reshape(1, d), w_pw2, b_pw2.reshape(1, d))


def _ffn_kernel(x_ref, mod_ref, g_ref, wg_ref, wu_ref, wd_ref, fg_ref, o_ref, h_ref, s_ref,
                *, d_ff, final_norm):
    x = x_ref[0]
    m = mod_ref[0]
    h_ref[...] = _rms_mod(x, g_ref[...], m[3:4], m[4:5]).astype(jnp.bfloat16)
    for c0 in range(0, d_ff, FF_CHUNK):
        h = h_ref[...]
        a = jnp.dot(h, wg_ref[:, c0:c0 + FF_CHUNK], preferred_element_type=jnp.float32)
        b = jnp.dot(h, wu_ref[:, c0:c0 + FF_CHUNK], preferred_element_type=jnp.float32)
        s_ref[:, c0:c0 + FF_CHUNK] = (a * _sigmoid(a) * b).astype(jnp.bfloat16)
    y = jnp.dot(s_ref[...], wd_ref[...], preferred_element_type=jnp.float32)
    out = x + m[5:6] * y
    if final_norm:
        ms = jnp.mean(out * out, axis=-1, keepdims=True)
        out = out * lax.rsqrt(ms + RMS_EPS) * fg_ref[...]
    o_ref[0] = out


def _ffn(x, modl, g, w_gate, w_up, w_down, final_g, index, *, tm, final_norm):
    bsz, seq, d = x.shape
    d_ff = w_gate.shape[2]
    kern = functools.partial(_ffn_kernel, d_ff=d_ff, final_norm=final_norm)
    return pl.pallas_call(
        kern,
        grid=(bsz, seq // tm),
        in_specs=[
            pl.BlockSpec((1, tm, d), lambda b, i: (b, i, 0)),
            pl.BlockSpec((1, N_MOD, d), lambda b, i: (b, 0, 0)),
            _const_spec((1, d)),
            _layer_spec(w_gate, index),
            _layer_spec(w_up, index),
            _layer_spec(w_down, index),
            _const_spec((1, d)),
        ],
        out_specs=pl.BlockSpec((1, tm, d), lambda b, i: (b, i, 0)),
        out_shape=jax.ShapeDtypeStruct(x.shape, x.dtype),
        scratch_shapes=[pltpu.VMEM((tm, d), jnp.bfloat16), pltpu.VMEM((tm, d_ff), jnp.bfloat16)],
        compiler_params=pltpu.CompilerParams(
            dimension_semantics=("parallel", "parallel"),
            vmem_limit_bytes=VMEM_LIMIT_BYTES),
        name="ffn",
    )(x, modl, g.reshape(1, d), w_gate, w_up, w_down, final_g.reshape(1, d))


def kernel(x, c, norm_mix_g, norm_ffn_g, w_mod, b_mod, ab_w_in, ab_conv, ab_w_pool, ab_pool_scale, ab_w_out, cf_w_pw1, cf_b_pw1, cf_w_dw, cf_b_dw, cf_ln_g, cf_ln_b, cf_w_pw2, cf_b_pw2, ffn_w_gate, ffn_w_up, ffn_w_down, final_norm_g):
    depth = w_mod.shape[0]
    bsz, seq, d = x.shape
    bf16 = jnp.bfloat16
    mod = _modulation(c, w_mod, b_mod).reshape(depth, bsz, N_MOD, d)
    ab_w_in, ab_w_pool, ab_w_out, cf_w_pw1, cf_w_pw2, ffn_w_gate, ffn_w_up, ffn_w_down = (
        w.astype(bf16) for w in (ab_w_in, ab_w_pool, ab_w_out, cf_w_pw1, cf_w_pw2,
                                 ffn_w_gate, ffn_w_up, ffn_w_down))
    for layer in range(depth):
        i = layer // 2
        if layer % 2 == 0:
            x = _ab_mixer(x, mod[layer], norm_mix_g[layer], ab_w_in, ab_conv[i], ab_w_pool,
                          ab_pool_scale[i], ab_w_out, i, tm=AB_ROW_TILE)
        else:
            x = _cf_mixer(x, mod[layer], norm_mix_g[layer], cf_w_pw1, cf_b_pw1[i], cf_w_dw[i],
                          cf_b_dw[i], cf_ln_g[i], cf_ln_b[i], cf_w_pw2, cf_b_pw2[i], i,
                          tm=CF_ROW_TILE)
        x = _ffn(x, mod[layer], norm_ffn_g[layer], ffn_w_gate, ffn_w_up, ffn_w_down, final_norm_g,
                 layer, tm=FFN_ROW_TILE, final_norm=(layer == depth - 1))
    return x
```

```python
import functools

import jax
import jax.numpy as jnp
from jax import lax
from jax.experimental import pallas as pl
from jax.experimental.pallas import tpu as pltpu

RMS_EPS = 1e-6
LN_EPS = 1e-5
POOL_WINDOWS = (2, 4, 8, 16)
N_MOD = 6

VMEM_LIMIT_BYTES = 56 * 1024 * 1024
SUBLANES = 8
LANES = 128

AB_ROW_TILE = 1024
AB_SUB_TILES = 2
CF_ROW_TILE = 1024
FFN_ROW_TILE = 1024
HALO_AB = 8
HALO_CF = 16
CF_ROW_CHUNK = 1024
FF_CHUNK = 256


def _const_spec(shape):
    nd = len(shape)
    return pl.BlockSpec(shape, lambda b, i: (0,) * nd, pipeline_mode=pl.Buffered(1))


def _layer_spec(stacked, index):
    tail = stacked.shape[1:]
    return pl.BlockSpec((None,) + tail, lambda b, i: (index,) + (0,) * len(tail),
                        pipeline_mode=pl.Buffered(1))


def _rms_mod(x, g, shift, scale):
    ms = jnp.mean(x * x, axis=-1, keepdims=True)
    return x * lax.rsqrt(ms + RMS_EPS) * (g * (1.0 + scale)) + shift


NEG_LOG2_E = -1.4426950408889634


def _sigmoid(x):
    return 1.0 / (1.0 + jnp.exp2(x * NEG_LOG2_E))


def _halo_specs(tm, halo, seq, d):
    nb = tm // halo
    last = seq // halo - 1
    main = pl.BlockSpec((1, tm, d), lambda b, i: (b, i, 0))
    left = pl.BlockSpec((1, halo, d), lambda b, i: (b, jnp.maximum(i * nb - 1, 0), 0))
    right = pl.BlockSpec((1, halo, d), lambda b, i: (b, jnp.minimum((i + 1) * nb, last), 0))
    return main, left, right


def _mod_kernel(c_ref, w_ref, b_ref, o_ref):
    c = c_ref[...]
    c_act = c * _sigmoid(c)
    o_ref[0] = jnp.dot(c_act, w_ref[0], preferred_element_type=jnp.float32) + b_ref[0]


def _modulation(c, w_mod, b_mod):
    depth, d, n = w_mod.shape
    bsz = c.shape[0]
    tn = 1024
    return pl.pallas_call(
        _mod_kernel,
        grid=(depth, n // tn),
        in_specs=[
            pl.BlockSpec((bsz, d), lambda l, j: (0, 0)),
            pl.BlockSpec((1, d, tn), lambda l, j: (l, 0, j)),
            pl.BlockSpec((1, 1, tn), lambda l, j: (l, 0, j)),
        ],
        out_specs=pl.BlockSpec((1, bsz, tn), lambda l, j: (l, 0, j)),
        out_shape=jax.ShapeDtypeStruct((depth, bsz, n), jnp.float32),
        compiler_params=pltpu.CompilerParams(
            dimension_semantics=("arbitrary", "arbitrary"),
            vmem_limit_bytes=VMEM_LIMIT_BYTES),
        name="adaln_mod",
    )(c, w_mod, b_mod.reshape(depth, 1, n))


def _ab_kernel(xm_ref, xl_ref, xr_ref, mod_ref, g_ref, win_ref, conv_ref, wpool_ref,
               pscale_ref, wout_ref, o_ref, ycat_ref, sh_ref, *, tm, seq, d_a, b_group):
    i = pl.program_id(1)
    halo = HALO_AB
    hm = tm // AB_SUB_TILES
    rows = hm + 2 * halo
    n_a = d_a // LANES
    m = mod_ref[0]

    gates = []
    for part in range(AB_SUB_TILES):
        lo = part * hm
        pieces = []
        if lo == 0:
            pieces.append(xl_ref[0])
        pieces.append(xm_ref[0, max(lo - halo, 0):min(lo + hm + halo, tm), :])
        if lo + hm == tm:
            pieces.append(xr_ref[0])
        xe = jnp.concatenate(pieces, axis=0) if len(pieces) > 1 else pieces[0]
        h = _rms_mod(xe, g_ref[...], m[0:1], m[1:2]).astype(jnp.bfloat16)
        u = jnp.dot(h, win_ref[...], preferred_element_type=jnp.float32)

        row = lax.broadcasted_iota(jnp.int32, (rows, 1), 0) + (i * tm + lo - halo)
        valid = (row >= 0) & (row < seq)

        for j in range(n_a):
            cvj = (u[:, d_a + j * LANES:d_a + (j + 1) * LANES]
                   * u[:, 2 * d_a + j * LANES:2 * d_a + (j + 1) * LANES])
            sh_ref[part, j] = jnp.where(valid, cvj, 0.0)
        for g in range(len(POOL_WINDOWS)):
            c0 = 3 * d_a + g * b_group
            sh_ref[part, n_a + g] = jnp.where(valid, u[:, c0:c0 + b_group], 0.0)

        gates.append(u[halo:halo + hm, 0:d_a])

    for part in range(AB_SUB_TILES):
        lo = part * hm
        for j in range(n_a):
            cols = slice(j * LANES, (j + 1) * LANES)
            conv = (conv_ref[0:1, cols] * sh_ref[part, j, halo - 1:halo - 1 + hm, :]
                    + conv_ref[1:2, cols] * sh_ref[part, j, halo:halo + hm, :]
                    + conv_ref[2:3, cols] * sh_ref[part, j, halo + 1:halo + 1 + hm, :])
            ycat_ref[lo:lo + hm, cols] = (gates[part][:, cols] * conv).astype(jnp.bfloat16)

        t = lax.broadcasted_iota(jnp.int32, (hm, 1), 0) + (i * tm + lo)
        for g, w in enumerate(POOL_WINDOWS):
            left = w // 2
            right = w - 1 - left
            s = sh_ref[part, n_a + g, halo - left:halo - left + hm, :]
            for j in range(-left + 1, right + 1):
                s = s + sh_ref[part, n_a + g, halo + j:halo + j + hm, :]
            cnt = (jnp.minimum(t + right, seq - 1) - jnp.maximum(t - left, 0) + 1).astype(jnp.float32)
            pooled = s * (1.0 / cnt) - sh_ref[part, n_a + g, halo:halo + hm, :]
            yb = jnp.dot(pooled.astype(jnp.bfloat16), wpool_ref[g], preferred_element_type=jnp.float32)
            yb = yb * pscale_ref[:, g * b_group:(g + 1) * b_group]
            ycat_ref[lo:lo + hm, d_a + g * b_group:d_a + (g + 1) * b_group] = yb.astype(jnp.bfloat16)

        y = jnp.dot(ycat_ref[lo:lo + hm, :], wout_ref[...], preferred_element_type=jnp.float32)
        o_ref[0, lo:lo + hm, :] = xm_ref[0, lo:lo + hm, :] + m[2:3] * y


def _ab_mixer(x, modl, g, w_in, conv_a, w_pool, pool_scale, w_out, index, *, tm):
    bsz, seq, d = x.shape
    d_a = conv_a.shape[1]
    _, n_groups, b_group, _ = w_pool.shape
    d_cat = w_out.shape[1]
    assert b_group == LANES and d_a % LANES == 0 and n_groups == len(POOL_WINDOWS)
    main, left, right = _halo_specs(tm, HALO_AB, seq, d)
    kern = functools.partial(_ab_kernel, tm=tm, seq=seq, d_a=d_a, b_group=b_group)
    return pl.pallas_call(
        kern,
        grid=(bsz, seq // tm),
        in_specs=[
            main, left, right,
            pl.BlockSpec((1, N_MOD, d), lambda b, i: (b, 0, 0)),
            _const_spec((1, d)),
            _layer_spec(w_in, index),
            _const_spec(conv_a.shape),
            _layer_spec(w_pool, index),
            _const_spec((1, n_groups * b_group)),
            _layer_spec(w_out, index),
        ],
        out_specs=pl.BlockSpec((1, tm, d), lambda b, i: (b, i, 0)),
        out_shape=jax.ShapeDtypeStruct(x.shape, x.dtype),
        scratch_shapes=[pltpu.VMEM((tm, d_cat), jnp.bfloat16),
                        pltpu.VMEM((AB_SUB_TILES, d_a // LANES + n_groups,
                                    tm // AB_SUB_TILES + 2 * HALO_AB, LANES), jnp.float32)],
        compiler_params=pltpu.CompilerParams(
            dimension_semantics=("parallel", "parallel"),
            vmem_limit_bytes=VMEM_LIMIT_BYTES),
        name="ab_mixer",
    )(x, x, x, modl, g.reshape(1, d), w_in, conv_a, w_pool, pool_scale.reshape(1, -1), w_out)


def _cf_kernel(xm_ref, xl_ref, xr_ref, mod_ref, g_ref, w1_ref, b1_ref, wdw_ref, bdw_ref,
               lng_ref, lnb_ref, w2_ref, b2_ref, o_ref, z_ref, c_ref, *, tm, seq, d, taps):
    i = pl.program_id(1)
    halo = HALO_CF
    rows = tm + 2 * halo
    reach = (taps - 1) // 2
    first = halo - reach
    n_chunks = tm // CF_ROW_CHUNK
    n_cb = d // LANES
    m = mod_ref[0]

    def glu_rows(lo, hi):
        parts = []
        if lo < halo:
            parts.append(xl_ref[0, lo:halo, :])
        parts.append(xm_ref[0, max(lo - halo, 0):min(hi - halo, tm), :])
        if hi > halo + tm:
            parts.append(xr_ref[0, 0:hi - halo - tm, :])
        xe = parts[0] if len(parts) == 1 else jnp.concatenate(parts, axis=0)
        h = _rms_mod(xe, g_ref[...], m[0:1], m[1:2]).astype(jnp.bfloat16)
        u = jnp.dot(h, w1_ref[...], preferred_element_type=jnp.float32) + b1_ref[...]
        row = lax.broadcasted_iota(jnp.int32, (hi - lo, 1), 0) + (i * tm - halo + lo)
        valid = (row >= 0) & (row < seq)
        z = jnp.where(valid, u[:, :d] * _sigmoid(u[:, d:]), 0.0)
        for j in range(n_cb):
            z_ref[j, lo:hi, :] = z[:, j * LANES:(j + 1) * LANES]

    def conv_rows(lo):
        hi = lo + CF_ROW_CHUNK
        sub = 4 * SUBLANES
        for r0 in range(lo, hi, sub):
            for j in range(n_cb):
                acc = None
                for k in range(taps):
                    term = (z_ref[j, r0 + first + k:r0 + first + k + sub, :]
                            * wdw_ref[k:k + 1, j * LANES:(j + 1) * LANES])
                    acc = term if acc is None else acc + term
                c_ref[r0:r0 + sub, j * LANES:(j + 1) * LANES] = acc
        cz = c_ref[lo:hi, :] + bdw_ref[...]
        mu = jnp.mean(cz, axis=-1, keepdims=True)
        dz = cz - mu
        var = jnp.mean(dz * dz, axis=-1, keepdims=True)
        ln = dz * lax.rsqrt(var + LN_EPS) * lng_ref[...] + lnb_ref[...]
        act = (ln * _sigmoid(ln)).astype(jnp.bfloat16)
        y = jnp.dot(act, w2_ref[...], preferred_element_type=jnp.float32) + b2_ref[...]
        o_ref[0, lo:hi, :] = xm_ref[0, lo:hi, :] + m[2:3] * y

    bounds = [0] + [2 * halo + CF_ROW_CHUNK * (c + 1) for c in range(n_chunks - 1)] + [rows]
    glu_rows(bounds[0], bounds[1])
    for c in range(n_chunks):
        if c + 1 < n_chunks:
            glu_rows(bounds[c + 1], bounds[c + 2])
        conv_rows(c * CF_ROW_CHUNK)


def _cf_mixer(x, modl, g, w_pw1, b_pw1, w_dw, b_dw, ln_g, ln_b, w_pw2, b_pw2, index, *, tm):
    bsz, seq, d = x.shape
    taps = w_dw.shape[0]
    main, left, right = _halo_specs(tm, HALO_CF, seq, d)
    kern = functools.partial(_cf_kernel, tm=tm, seq=seq, d=d, taps=taps)
    rows = tm + 2 * HALO_CF
    return pl.pallas_call(
        kern,
        grid=(bsz, seq // tm),
        in_specs=[
            main, left, right,
            pl.BlockSpec((1, N_MOD, d), lambda b, i: (b, 0, 0)),
            _const_spec((1, d)),
            _layer_spec(w_pw1, index),
            _const_spec((1, 2 * d)),
            _const_spec((taps, d)),
            _const_spec((1, d)),
            _const_spec((1, d)),
            _const_spec((1, d)),
            _layer_spec(w_pw2, index),
            _const_spec((1, d)),
        ],
        out_specs=pl.BlockSpec((1, tm, d), lambda b, i: (b, i, 0)),
        out_shape=jax.ShapeDtypeStruct(x.shape, x.dtype),
        scratch_shapes=[pltpu.VMEM((d // LANES, rows, LANES), jnp.float32),
                        pltpu.VMEM((tm, d), jnp.float32)],
        compiler_params=pltpu.CompilerParams(
            dimension_semantics=("parallel", "parallel"),
            vmem_limit_bytes=VMEM_LIMIT_BYTES),
        name="cf_mixer",
    )(x, x, x, modl, g.reshape(1, d), w_pw1, b_pw1.reshape(1, -1), w_dw, b_dw.reshape(1, d),
      ln_g.reshape(1, d), ln_b.reshape(1, d), w_pw2, b_pw2.reshape(1, d))


def _ffn_kernel(x_ref, mod_ref, g_ref, wg_ref, wu_ref, wd_ref, fg_ref, o_ref, h_ref, s_ref,
                *, d_ff, final_norm):
    x = x_ref[0]
    m = mod_ref[0]
    h_ref[...] = _rms_mod(x, g_ref[...], m[3:4], m[4:5]).astype(jnp.bfloat16)
    for c0 in range(0, d_ff, FF_CHUNK):
        h = h_ref[...]
        a = jnp.dot(h, wg_ref[:, c0:c0 + FF_CHUNK], preferred_element_type=jnp.float32)
        b = jnp.dot(h, wu_ref[:, c0:c0 + FF_CHUNK], preferred_element_type=jnp.float32)
        s_ref[:, c0:c0 + FF_CHUNK] = (a * _sigmoid(a) * b).astype(jnp.bfloat16)
    y = jnp.dot(s_ref[...], wd_ref[...], preferred_element_type=jnp.float32)
    out = x + m[5:6] * y
    if final_norm:
        ms = jnp.mean(out * out, axis=-1, keepdims=True)
        out = out * lax.rsqrt(ms + RMS_EPS) * fg_ref[...]
    o_ref[0] = out


def _ffn(x, modl, g, w_gate, w_up, w_down, final_g, index, *, tm, final_norm):
    bsz, seq, d = x.shape
    d_ff = w_gate.shape[2]
    kern = functools.partial(_ffn_kernel, d_ff=d_ff, final_norm=final_norm)
    return pl.pallas_call(
        kern,
        grid=(bsz, seq // tm),
        in_specs=[
            pl.BlockSpec((1, tm, d), lambda b, i: (b, i, 0)),
            pl.BlockSpec((1, N_MOD, d), lambda b, i: (b, 0, 0)),
            _const_spec((1, d)),
            _layer_spec(w_gate, index),
            _layer_spec(w_up, index),
            _layer_spec(w_down, index),
            _const_spec((1, d)),
        ],
        out_specs=pl.BlockSpec((1, tm, d), lambda b, i: (b, i, 0)),
        out_shape=jax.ShapeDtypeStruct(x.shape, x.dtype),
        scratch_shapes=[pltpu.VMEM((tm, d), jnp.bfloat16), pltpu.VMEM((tm, d_ff), jnp.bfloat16)],
        compiler_params=pltpu.CompilerParams(
            dimension_semantics=("parallel", "parallel"),
            vmem_limit_bytes=VMEM_LIMIT_BYTES),
        name="ffn",
    )(x, modl, g.reshape(1, d), w_gate, w_up, w_down, final_g.reshape(1, d))


def kernel(x, c, norm_mix_g, norm_ffn_g, w_mod, b_mod, ab_w_in, ab_conv, ab_w_pool, ab_pool_scale, ab_w_out, cf_w_pw1, cf_b_pw1, cf_w_dw, cf_b_dw, cf_ln_g, cf_ln_b, cf_w_pw2, cf_b_pw2, ffn_w_gate, ffn_w_up, ffn_w_down, final_norm_g):
    depth = w_mod.shape[0]
    bsz, seq, d = x.shape
    bf16 = jnp.bfloat16
    mod = _modulation(c, w_mod, b_mod).reshape(depth, bsz, N_MOD, d)
    ab_w_in, ab_w_pool, ab_w_out, cf_w_pw1, cf_w_pw2, ffn_w_gate, ffn_w_up, ffn_w_down = (
        w.astype(bf16) for w in (ab_w_in, ab_w_pool, ab_w_out, cf_w_pw1, cf_w_pw2,
                                 ffn_w_gate, ffn_w_up, ffn_w_down))
    for layer in range(depth):
        i = layer // 2
        if layer % 2 == 0:
            x = _ab_mixer(x, mod[layer], norm_mix_g[layer], ab_w_in, ab_conv[i], ab_w_pool,
                          ab_pool_scale[i], ab_w_out, i, tm=AB_ROW_TILE)
        else:
            x = _cf_mixer(x, mod[layer], norm_mix_g[layer], cf_w_pw1, cf_b_pw1[i], cf_w_dw[i],
                          cf_b_dw[i], cf_ln_g[i], cf_ln_b[i], cf_w_pw2, cf_b_pw2[i], i,
                          tm=CF_ROW_TILE)
        x = _ffn(x, mod[layer], norm_ffn_g[layer], ffn_w_gate, ffn_w_up, ffn_w_down, final_norm_g,
                 layer, tm=FFN_ROW_TILE, final_norm=(layer == depth - 1))
    return x
```
